```python
import math, functools
import jax, jax.numpy as jnp
from jax import lax
import numpy as np

D_MODEL = 1024
BATCH = 8
SEQ = 8192
DEPTH = 1
DEC_BATCH = 128
DEC_SEQ = 8
PAST_LEN = 8192
PAGE_SIZE = 128

A_HEADS = 8
A_HEAD_DIM = 64
A_WIDTH = A_HEADS * A_HEAD_DIM
MOBA_BLOCK = 256
MOBA_TOPK = 3
Q_BLOCK = 128
B_HEADS = 8
B_HEAD_DIM = 64
B_WIDTH = B_HEADS * B_HEAD_DIM
CONV_W = 4
GDN_CHUNK = 64
M_HEADS = 4
M_HEAD_DIM = 128
M_WIDTH = M_HEADS * M_HEAD_DIM
N_MEM = 256
N_BRANCH = 3
IN_SIZES = (A_WIDTH, A_WIDTH, A_WIDTH, B_WIDTH, B_WIDTH, B_WIDTH, B_WIDTH, B_HEADS, B_HEADS, M_WIDTH, N_BRANCH * D_MODEL)
N_IN = 3 * A_WIDTH + 4 * B_WIDTH + 2 * B_HEADS + M_WIDTH + N_BRANCH * D_MODEL
PEER_HEADS = 8
PEER_NKEYS = 128
PEER_N_EXPERTS = PEER_NKEYS * PEER_NKEYS
PEER_QDIM = 256
PEER_HALF = PEER_QDIM // 2
PEER_TOPK = 16
PEER_BLOCK = 128
EPS = 1e-6
NEG = -1e30

kernel_name = 'hybrid_moba_gdn_peer_step'


def rmsnorm(x, g):
    xf = x.astype(jnp.float32)
    y = xf * lax.rsqrt(jnp.mean(xf * xf, axis=-1, keepdims=True) + EPS)
    return (y * g.astype(jnp.float32)).astype(x.dtype)


def l2norm(x):
    xf = x.astype(jnp.float32)
    return xf * lax.rsqrt(jnp.sum(xf * xf, axis=-1, keepdims=True) + EPS)


def split_cols(x, sizes):
    out, start = [], 0
    for sz in sizes:
        out.append(x[..., start:start + sz])
        start += sz
    return out


def short_conv(x, buf, w):
    t = x.shape[1]
    xp = jnp.concatenate([buf.astype(x.dtype), x], axis=1)
    y = xp[:, 0:t] * w[0]
    for i in range(1, CONV_W):
        y = y + xp[:, i:i + t] * w[i]
    return jax.nn.silu(y), xp[:, xp.shape[1] - (CONV_W - 1):]


def gdn_chunked(q, k, v, g, beta, m0, chunk):
    n, t, h, dk = q.shape
    dv = v.shape[-1]
    nc = t // chunk

    def to_chunks(a):
        a = a.reshape((n, nc, chunk, h) + a.shape[3:])
        return jnp.moveaxis(a, (1, 3), (0, 2))

    qc, kc, vc, gc, bc = to_chunks(q), to_chunks(k), to_chunks(v), to_chunks(g), to_chunks(beta)
    G = jnp.cumsum(gc, axis=-1)
    tri_incl = jnp.tril(jnp.ones((chunk, chunk), bool))
    tri_strict = jnp.tril(jnp.ones((chunk, chunk), bool), -1)
    diff = G[..., :, None] - G[..., None, :]
    decay = jnp.where(tri_incl, jnp.exp(jnp.where(tri_incl, diff, 0.0)), 0.0)
    kk = jnp.einsum('...tk,...jk->...tj', kc, kc)
    L = jnp.where(tri_strict, bc[..., :, None] * kk * decay, 0.0)
    eye = jnp.eye(chunk, dtype=L.dtype)
    rhs = jnp.concatenate([(bc * jnp.exp(G))[..., None] * kc, bc[..., None] * vc], axis=-1)
    sol = lax.linalg.triangular_solve(eye + L, rhs, left_side=True, lower=True, unit_diagonal=True)
    W, Ub = sol[..., :dk], sol[..., dk:]
    qk = jnp.where(tri_incl, jnp.einsum('...tk,...jk->...tj', qc, kc) * decay, 0.0)
    gq = jnp.exp(G)[..., None] * qc
    kend = jnp.exp(G[..., -1:] - G)[..., None] * kc
    gend = jnp.exp(G[..., -1])

    def step(m, xs):
        w_, ub_, qk_, gq_, kend_, gend_ = xs
        u = ub_ - jnp.einsum('nhck,nhkv->nhcv', w_, m)
        o = jnp.einsum('nhck,nhkv->nhcv', gq_, m) + jnp.einsum('nhcj,nhjv->nhcv', qk_, u)
        m = gend_[..., None, None] * m + jnp.einsum('nhck,nhcv->nhkv', kend_, u)
        return m, o

    m_final, o = lax.scan(step, m0, (W, Ub, qk, gq, kend, gend))
    o = jnp.moveaxis(o, (0, 2), (1, 3)).reshape(n, t, h, dv)
    return o, m_final


def moba_prompt(q, k, v):
    n, s, h, dh = q.shape
    nb = -(-s // MOBA_BLOCK)
    pad = nb * MOBA_BLOCK - s
    kb = jnp.pad(k, ((0, 0), (0, pad), (0, 0), (0, 0))).reshape(n, nb, MOBA_BLOCK, h, dh).transpose(0, 1, 3, 2, 4)
    vb = jnp.pad(v, ((0, 0), (0, pad), (0, 0), (0, 0))).reshape(n, nb, MOBA_BLOCK, h, dh).transpose(0, 1, 3, 2, 4)
    kmean = jnp.mean(kb.astype(jnp.float32), axis=3)
    n_sel = min(MOBA_TOPK, nb - 1)
    nq = s // Q_BLOCK
    qb = q.reshape(n, nq, Q_BLOCK, h, dh)
    scale = dh ** -0.5
    hidx = jnp.arange(h)[:, None, None]
    blk_ids = jnp.arange(nb)

    def one(idx):
        bi = idx // nq
        ci = idx % nq
        qc = qb[bi, ci]
        qpos = ci * Q_BLOCK + jnp.arange(Q_BLOCK)
        own = (ci * Q_BLOCK) // MOBA_BLOCK
        k_own, v_own = kb[bi, own], vb[bi, own]
        kpos = own * MOBA_BLOCK + jnp.arange(MOBA_BLOCK)
        s_own = jnp.einsum('qhd,hkd->hqk', qc, k_own).astype(jnp.float32) * scale
        s_own = jnp.where(kpos[None, None, :] <= qpos[None, :, None], s_own, NEG)
        if n_sel == 0:
            p = jax.nn.softmax(s_own, axis=-1)
            o = jnp.einsum('hqk,hkd->qhd', p, v_own)
        else:
            bs = jnp.einsum('qhd,bhd->hqb', qc, kmean[bi]).astype(jnp.float32)
            bs = jnp.where(blk_ids[None, None, :] < own, bs, NEG)
            _, sel = lax.top_k(bs, n_sel)
            valid = sel < own
            k_sel = kb[bi, sel, hidx]
            v_sel = vb[bi, sel, hidx].reshape(h, Q_BLOCK, n_sel * MOBA_BLOCK, dh)
            s_sel = jnp.einsum('qhd,hqnkd->hqnk', qc, k_sel).astype(jnp.float32) * scale
            s_sel = jnp.where(valid[..., None], s_sel, NEG).reshape(h, Q_BLOCK, n_sel * MOBA_BLOCK)
            p = jax.nn.softmax(jnp.concatenate([s_sel, s_own], axis=-1), axis=-1)
            n_s = n_sel * MOBA_BLOCK
            o = jnp.einsum('hqk,hqkd->qhd', p[..., :n_s], v_sel) + jnp.einsum('hqk,hkd->qhd', p[..., n_s:], v_own)
        return o.astype(q.dtype)

    out = lax.map(one, jnp.arange(n * nq))
    return out.reshape(n, s, h, dh)


def moba_sample(q, k, v, pool_k, pool_v, page_table):
    n, t, h, dh = q.shape
    n_pages = page_table.shape[1]
    ppb = MOBA_BLOCK // PAGE_SIZE
    own = (n_pages * PAGE_SIZE) // MOBA_BLOCK
    r = n_pages - own * ppb
    n_sel = min(MOBA_TOPK, own)
    scale = dh ** -0.5
    causal = jnp.tril(jnp.ones((t, t), bool))
    page_mean = jnp.mean(pool_k.astype(jnp.float32), axis=2)
    hidx = jnp.arange(h)[:, None, None, None]

    def one(args):
        qn, kn, vn, pt = args
        k_sh, v_sh = kn.transpose(1, 0, 2), vn.transpose(1, 0, 2)
        mask_sh = causal
        if r > 0:
            po = pt[own * ppb:]
            k_op = pool_k[po].transpose(1, 0, 2, 3).reshape(h, r * PAGE_SIZE, dh)
            v_op = pool_v[po].transpose(1, 0, 2, 3).reshape(h, r * PAGE_SIZE, dh)
            k_sh = jnp.concatenate([k_op.astype(kn.dtype), k_sh], axis=1)
            v_sh = jnp.concatenate([v_op.astype(vn.dtype), v_sh], axis=1)
            mask_sh = jnp.concatenate([jnp.ones((t, r * PAGE_SIZE), bool), causal], axis=1)
        s_sh = jnp.einsum('qhd,hkd->hqk', qn, k_sh).astype(jnp.float32) * scale
        s_sh = jnp.where(mask_sh[None], s_sh, NEG)
        if n_sel == 0:
            p = jax.nn.softmax(s_sh, axis=-1)
            o = jnp.einsum('hqk,hkd->qhd', p, v_sh)
        else:
            pm = page_mean[pt[:own * ppb]].reshape(own, ppb, h, dh).mean(axis=1)
            bs = jnp.einsum('qhd,bhd->hqb', qn, pm).astype(jnp.float32)
            _, sel = lax.top_k(bs, n_sel)
            lp = sel[..., None] * ppb + jnp.arange(ppb)
            phys = pt[lp]
            k_sel = pool_k[phys, hidx].reshape(h, t, n_sel * MOBA_BLOCK, dh)
            v_sel = pool_v[phys, hidx].reshape(h, t, n_sel * MOBA_BLOCK, dh)
            s_sel = jnp.einsum('qhd,hqkd->hqk', qn, k_sel).astype(jnp.float32) * scale
            p = jax.nn.softmax(jnp.concatenate([s_sel, s_sh], axis=-1), axis=-1)
            n_s = n_sel * MOBA_BLOCK
            o = jnp.einsum('hqk,hqkd->qhd', p[..., :n_s], v_sel) + jnp.einsum('hqk,hkd->qhd', p[..., n_s:], v_sh)
        return o.astype(q.dtype)

    return lax.map(one, (q, k, v, page_table))


def mem_kv(mem, g_mem, w_mem_kv):
    n = mem.shape[0]
    mk, mv = split_cols(rmsnorm(mem, g_mem) @ w_mem_kv, (M_WIDTH, M_WIDTH))
    return mk.reshape(n, N_MEM, M_HEADS, M_HEAD_DIM), mv.reshape(n, N_MEM, M_HEADS, M_HEAD_DIM)


def mem_attend(q, mk, mv):
    s = jnp.einsum('nthd,nmhd->nhtm', q, mk).astype(jnp.float32) * (M_HEAD_DIM ** -0.5)
    p = jax.nn.softmax(s, axis=-1)
    return jnp.einsum('nhtm,nmhd->nthd', p, mv).astype(q.dtype)


def gdn_branch(bq, bk, bv, bz, bb, ba, conv_buf, m0, conv_w, a_log, dt_bias, g_out):
    n, t, _ = bq.shape
    qkv, conv_new = short_conv(jnp.concatenate([bq, bk, bv], axis=-1), conv_buf, conv_w)
    q, k, v = split_cols(qkv, (B_WIDTH, B_WIDTH, B_WIDTH))
    shp = (n, t, B_HEADS, B_HEAD_DIM)
    q = l2norm(q.reshape(shp)) * (B_HEAD_DIM ** -0.5)
    k = l2norm(k.reshape(shp))
    v = v.reshape(shp).astype(jnp.float32)
    beta = jax.nn.sigmoid(bb.astype(jnp.float32))
    g = -jnp.exp(a_log.astype(jnp.float32)) * jax.nn.softplus(ba.astype(jnp.float32) + dt_bias.astype(jnp.float32))
    chunk = GDN_CHUNK if t % GDN_CHUNK == 0 else t
    o, m_new = gdn_chunked(q, k, v, g, beta, m0.astype(jnp.float32), chunk)
    o = rmsnorm(o, g_out) * jax.nn.silu(bz.reshape(shp).astype(jnp.float32))
    return o.astype(bq.dtype), conv_new, m_new


def peer(h, w_pq, sub_k1, sub_k2, u_emb, v_emb):
    n, t, d = h.shape
    n_tok = n * t
    n_blk = -(-n_tok // PEER_BLOCK)
    xt = jnp.pad(h.reshape(n_tok, d), ((0, n_blk * PEER_BLOCK - n_tok), (0, 0))).reshape(n_blk, PEER_BLOCK, d)

    def one(xb):
        q = (xb @ w_pq).reshape(PEER_BLOCK, PEER_HEADS, 2, PEER_HALF)
        s1 = jnp.einsum('bhd,hkd->bhk', q[:, :, 0], sub_k1).astype(jnp.float32)
        s2 = jnp.einsum('bhd,hkd->bhk', q[:, :, 1], sub_k2).astype(jnp.float32)
        v1, i1 = lax.top_k(s1, PEER_TOPK)
        v2, i2 = lax.top_k(s2, PEER_TOPK)
        cand = (v1[..., :, None] + v2[..., None, :]).reshape(PEER_BLOCK, PEER_HEADS, PEER_TOPK * PEER_TOPK)
        cidx = (i1[..., :, None] * PEER_NKEYS + i2[..., None, :]).reshape(PEER_BLOCK, PEER_HEADS, PEER_TOPK * PEER_TOPK)
        sc, pos = lax.top_k(cand, PEER_TOPK)
        eidx = jnp.take_along_axis(cidx, pos, axis=-1)
        gate = jax.nn.softmax(sc, axis=-1)
        act = jax.nn.gelu(jnp.einsum('bd,bhkd->bhk', xb, u_emb[eidx]).astype(jnp.float32), approximate=False)
        return jnp.einsum('bhk,bhkd->bd', (gate * act).astype(xb.dtype), v_emb[eidx])

    out = lax.map(one, xt).reshape(n_blk * PEER_BLOCK, d)[:n_tok]
    return out.reshape(n, t, d).astype(h.dtype)


def layer(x, mk, mv, conv_buf, m0, attn_fn, w_in, g_attn, conv_w, a_log, dt_bias, g_gdn_out,
          w_oa, w_ob, w_om, w_out, g_ffn, w_peer_q, peer_k1, peer_k2, peer_u, peer_v):
    n, t, _ = x.shape
    h = rmsnorm(x, g_attn)
    aq, ak, av, bq, bk, bv, bz, bb, ba, mq, gl = split_cols(h @ w_in, IN_SIZES)
    a_shp = (n, t, A_HEADS, A_HEAD_DIM)
    ak, av = ak.reshape(a_shp), av.reshape(a_shp)
    oa = attn_fn(aq.reshape(a_shp), ak, av)
    ob, conv_new, m_new = gdn_branch(bq, bk, bv, bz, bb, ba, conv_buf, m0, conv_w, a_log, dt_bias, g_gdn_out)
    om = mem_attend(mq.reshape(n, t, M_HEADS, M_HEAD_DIM), mk, mv)
    gates = jax.nn.sigmoid(gl.astype(jnp.float32)).reshape(n, t, N_BRANCH, D_MODEL).astype(x.dtype)
    merged = (gates[:, :, 0] * (oa.reshape(n, t, A_WIDTH) @ w_oa)
              + gates[:, :, 1] * (ob.reshape(n, t, B_WIDTH) @ w_ob)
              + gates[:, :, 2] * (om.reshape(n, t, M_WIDTH) @ w_om))
    x = x + merged @ w_out
    x = x + peer(rmsnorm(x, g_ffn), w_peer_q, peer_k1, peer_k2, peer_u, peer_v)
    return x, ak, av, conv_new, m_new


def setup_inputs(seed: int = 0) -> dict:
    key = jax.random.key(seed)
    ks = jax.random.split(key, 32)
    f32 = jnp.float32
    n_pages = PAST_LEN // PAGE_SIZE
    n_pool = (5 * DEC_BATCH * n_pages + 3) // 4

    def nrm(k, shape, scale):
        return jax.random.normal(k, shape, f32) * scale

    page_table = jax.random.permutation(ks[8], n_pool)[:DEC_BATCH * n_pages].reshape(DEC_BATCH, n_pages).astype(jnp.int32)
    dt = jnp.exp(jax.random.uniform(ks[14], (DEPTH, B_HEADS), f32, math.log(1e-3), math.log(1e-1)))
    return {
        'x_prompt': nrm(ks[0], (BATCH, SEQ, D_MODEL), 1.0),
        'x_sample': nrm(ks[1], (DEC_BATCH, DEC_SEQ, D_MODEL), 1.0),
        'cache_moba_k': nrm(ks[2], (DEPTH, n_pool, A_HEADS, PAGE_SIZE, A_HEAD_DIM), 1.0),
        'cache_moba_v': nrm(ks[3], (DEPTH, n_pool, A_HEADS, PAGE_SIZE, A_HEAD_DIM), 1.0),
        'state_gdn': nrm(ks[4], (DEPTH, DEC_BATCH, B_HEADS, B_HEAD_DIM, B_HEAD_DIM), 0.1),
        'state_conv': nrm(ks[5], (DEPTH, DEC_BATCH, CONV_W - 1, 3 * B_WIDTH), 1.0),
        'cache_mem_k': nrm(ks[6], (DEPTH, DEC_BATCH, N_MEM, M_HEADS, M_HEAD_DIM), 1.0),
        'cache_mem_v': nrm(ks[7], (DEPTH, DEC_BATCH, N_MEM, M_HEADS, M_HEAD_DIM), 1.0),
        'page_table': page_table,
        'mem_prompt': nrm(ks[9], (BATCH, N_MEM, D_MODEL), 1.0),
        'w_in': nrm(ks[10], (DEPTH, D_MODEL, N_IN), D_MODEL ** -0.5),
        'g_attn': 1.0 + nrm(ks[11], (DEPTH, D_MODEL), 0.02),
        'conv_w': nrm(ks[12], (DEPTH, CONV_W, 3 * B_WIDTH), CONV_W ** -0.5),
        'a_log': jnp.log(jax.random.uniform(ks[13], (DEPTH, B_HEADS), f32, 1.0, 16.0)),
        'dt_bias': dt + jnp.log(-jnp.expm1(-dt)),
        'g_gdn_out': 1.0 + nrm(ks[15], (DEPTH, B_HEAD_DIM), 0.02),
        'w_oa': nrm(ks[16], (DEPTH, A_WIDTH, D_MODEL), A_WIDTH ** -0.5),
        'w_ob': nrm(ks[17], (DEPTH, B_WIDTH, D_MODEL), B_WIDTH ** -0.5),
        'w_om': nrm(ks[18], (DEPTH, M_WIDTH, D_MODEL), M_WIDTH ** -0.5),
        'w_out': nrm(ks[19], (DEPTH, D_MODEL, D_MODEL), D_MODEL ** -0.5),
        'g_mem': 1.0 + nrm(ks[20], (DEPTH, D_MODEL), 0.02),
        'w_mem_kv': nrm(ks[21], (DEPTH, D_MODEL, 2 * M_WIDTH), D_MODEL ** -0.5),
        'g_ffn': 1.0 + nrm(ks[22], (DEPTH, D_MODEL), 0.02),
        'w_peer_q': nrm(ks[23], (DEPTH, D_MODEL, PEER_HEADS * PEER_QDIM), D_MODEL ** -0.5),
        'peer_k1': nrm(ks[24], (DEPTH, PEER_HEADS, PEER_NKEYS, PEER_HALF), PEER_HALF ** -0.5),
        'peer_k2': nrm(ks[25], (DEPTH, PEER_HEADS, PEER_NKEYS, PEER_HALF), PEER_HALF ** -0.5),
        'peer_u': nrm(ks[26], (DEPTH, PEER_N_EXPERTS, D_MODEL), D_MODEL ** -0.5),
        'peer_v': nrm(ks[27], (DEPTH, PEER_N_EXPERTS, D_MODEL), 0.5),
        'g_final': 1.0 + nrm(ks[28], (D_MODEL,), 0.02),
    }


def reference(x_prompt, x_sample, cache_moba_k, cache_moba_v, state_gdn, state_conv,
              cache_mem_k, cache_mem_v, page_table, mem_prompt,
              w_in, g_attn, conv_w, a_log, dt_bias, g_gdn_out, w_oa, w_ob, w_om, w_out,
              g_mem, w_mem_kv, g_ffn, w_peer_q, peer_k1, peer_k2, peer_u, peer_v, g_final):
    xp, xs = x_prompt, x_sample
    n_p, t_p = xp.shape[0], xp.shape[1]
    pk, pv, pS, pc, pmk, pmv = [], [], [], [], [], []
    sk, sv, sS, sc = [], [], [], []
    for l in range(DEPTH):
        wl = (w_in[l], g_attn[l], conv_w[l], a_log[l], dt_bias[l], g_gdn_out[l], w_oa[l], w_ob[l], w_om[l],
              w_out[l], g_ffn[l], w_peer_q[l], peer_k1[l], peer_k2[l], peer_u[l], peer_v[l])
        mk, mv = mem_kv(mem_prompt, g_mem[l], w_mem_kv[l])
        conv0 = jnp.zeros((n_p, CONV_W - 1, 3 * B_WIDTH), xp.dtype)
        m0 = jnp.zeros((n_p, B_HEADS, B_HEAD_DIM, B_HEAD_DIM), jnp.float32)
        xp, ak, av, c_new, m_new = layer(xp, mk, mv, conv0, m0, moba_prompt, *wl)
        pk.append(ak.reshape(n_p, t_p // PAGE_SIZE, PAGE_SIZE, A_HEADS, A_HEAD_DIM).transpose(0, 1, 3, 2, 4))
        pv.append(av.reshape(n_p, t_p // PAGE_SIZE, PAGE_SIZE, A_HEADS, A_HEAD_DIM).transpose(0, 1, 3, 2, 4))
        pS.append(m_new)
        pc.append(c_new)
        pmk.append(mk)
        pmv.append(mv)
        attn_s = functools.partial(moba_sample, pool_k=cache_moba_k[l], pool_v=cache_moba_v[l], page_table=page_table)
        xs, ak, av, c_new, m_new = layer(xs, cache_mem_k[l], cache_mem_v[l], state_conv[l], state_gdn[l], attn_s, *wl)
        sk.append(ak.transpose(0, 2, 1, 3))
        sv.append(av.transpose(0, 2, 1, 3))
        sS.append(m_new)
        sc.append(c_new)
    y_prompt = rmsnorm(xp, g_final)
    y_sample = rmsnorm(xs, g_final)
    p_moba_k = jnp.stack(pk, axis=0)
    p_moba_v = jnp.stack(pv, axis=0)
    p_gdn = jnp.stack(pS, axis=0)
    p_conv = jnp.stack(pc, axis=0)
    p_mem_k = jnp.stack(pmk, axis=0)
    p_mem_v = jnp.stack(pmv, axis=0)
    s_moba_k = jnp.stack(sk, axis=0)
    s_moba_v = jnp.stack(sv, axis=0)
    s_gdn = jnp.stack(sS, axis=0)
    s_conv = jnp.stack(sc, axis=0)
    return (y_prompt, y_sample, p_moba_k, p_moba_v, p_gdn, p_conv, p_mem_k, p_mem_v, s_moba_k, s_moba_v, s_gdn, s_conv)
```

```python
import functools
import math

import jax
import jax.numpy as jnp
from jax import lax
from jax.experimental import pallas as pl
from jax.experimental.pallas import tpu as pltpu

F32 = jnp.float32
BF16 = jnp.bfloat16
EPS = 1e-6
NEG = -1e30
REMOVED = -3e38
HI = lax.Precision.HIGHEST

V7X_LANES = 128
V7X_SUBLANES = 8
VMEM_LIMIT_BYTES = 48 * 1024 * 1024

A_HEADS, A_HEAD_DIM = 8, 64
MOBA_BLOCK, MOBA_TOPK = 256, 3
B_HEADS, B_HEAD_DIM = 8, 64
CONV_W = 4
GDN_CHUNK = 64
M_HEADS, M_HEAD_DIM = 4, 128
N_BRANCH = 3
PEER_HEADS, PEER_NKEYS, PEER_TOPK = 8, 128, 16
PEER_HALF = 128
NT = (((1,), (1,)), ((), ()))


def _tile(m, pref):
    t = pref
    while m % t:
        t //= 2
    return t


def _params(n_axes, vmem=VMEM_LIMIT_BYTES):
    return pltpu.CompilerParams(dimension_semantics=("arbitrary",) * n_axes, vmem_limit_bytes=vmem)


def _dot(a, b, precision=None):
    return jnp.dot(a, b, preferred_element_type=F32, precision=precision)


def _dot_nt(a, b, precision=None):
    return lax.dot_general(a, b, NT, preferred_element_type=F32, precision=precision)


def _softplus(x):
    return jnp.maximum(x, 0.0) + jnp.log1p(jnp.exp(-jnp.abs(x)))


def _sigmoid(x):
    return 1.0 / (1.0 + jnp.exp(-x))


def _rmsnorm_kernel(x_ref, g_ref, o_ref):
    x = x_ref[...]
    y = x * lax.rsqrt(jnp.mean(x * x, axis=-1, keepdims=True) + EPS)
    o_ref[...] = (y * g_ref[...]).astype(o_ref.dtype)


def _rmsnorm(x, g, out_dtype, tm=512):
    m, d = x.shape
    tm = _tile(m, tm)
    return pl.pallas_call(
        _rmsnorm_kernel,
        grid=(m // tm,),
        in_specs=[pl.BlockSpec((tm, d), lambda i: (i, 0)), pl.BlockSpec((1, d), lambda i: (0, 0))],
        out_specs=pl.BlockSpec((tm, d), lambda i: (i, 0)),
        out_shape=jax.ShapeDtypeStruct((m, d), out_dtype),
        compiler_params=_params(1),
        name="rmsnorm",
    )(x, g.reshape(1, d))


def _add_rmsnorm_kernel(x_ref, y_ref, g_ref, o_ref):
    x = x_ref[...] + y_ref[...]
    y = x * lax.rsqrt(jnp.mean(x * x, axis=-1, keepdims=True) + EPS)
    o_ref[...] = (y * g_ref[...]).astype(o_ref.dtype)


def _add_rmsnorm(x, y, g, tm=512):
    m, d = x.shape
    tm = _tile(m, tm)
    return pl.pallas_call(
        _add_rmsnorm_kernel,
        grid=(m // tm,),
        in_specs=[pl.BlockSpec((tm, d), lambda i: (i, 0)), pl.BlockSpec((tm, d), lambda i: (i, 0)),
                  pl.BlockSpec((1, d), lambda i: (0, 0))],
        out_specs=pl.BlockSpec((tm, d), lambda i: (i, 0)),
        out_shape=jax.ShapeDtypeStruct((m, d), F32),
        compiler_params=_params(1),
        name="add_rmsnorm",
    )(x, y, g.reshape(1, d))


def _matmul_kernel(x_ref, w_ref, o_ref):
    o_ref[...] = _dot(x_ref[...], w_ref[...]).astype(o_ref.dtype)


def _matmul(x, w, out_dtype, tm=512, tn=512):
    m, k = x.shape
    n = w.shape[1]
    tm, tn = _tile(m, tm), min(tn, n)
    return pl.pallas_call(
        _matmul_kernel,
        grid=(m // tm, n // tn),
        in_specs=[pl.BlockSpec((tm, k), lambda i, j: (i, 0)), pl.BlockSpec((k, tn), lambda i, j: (0, j))],
        out_specs=pl.BlockSpec((tm, tn), lambda i, j: (i, j)),
        out_shape=jax.ShapeDtypeStruct((m, n), out_dtype),
        compiler_params=_params(2),
        name="matmul",
    )(x, w)


def _mem_attn_kernel(q_ref, k_ref, v_ref, o_ref):
    scale = M_HEAD_DIM ** -0.5
    for h in range(M_HEADS):
        sl = slice(h * M_HEAD_DIM, (h + 1) * M_HEAD_DIM)
        s = _dot_nt(q_ref[0, :, sl], k_ref[0, :, sl]) * scale
        p = jnp.exp(s - jnp.max(s, axis=-1, keepdims=True))
        l = jnp.sum(p, axis=-1, keepdims=True)
        o = _dot(p.astype(BF16), v_ref[0, :, sl]) / l
        o_ref[0, :, sl] = o.astype(o_ref.dtype)


def _mem_attn(q, mk, mv, tq):
    n, t, w = q.shape
    nm = mk.shape[1]
    tq = min(tq, t)
    return pl.pallas_call(
        _mem_attn_kernel,
        grid=(n, t // tq),
        in_specs=[pl.BlockSpec((1, tq, w), lambda b, i: (b, i, 0)),
                  pl.BlockSpec((1, nm, w), lambda b, i: (b, 0, 0)),
                  pl.BlockSpec((1, nm, w), lambda b, i: (b, 0, 0))],
        out_specs=pl.BlockSpec((1, tq, w), lambda b, i: (b, i, 0)),
        out_shape=jax.ShapeDtypeStruct((n, t, w), BF16),
        compiler_params=_params(2),
        name="mem_attn",
    )(q, mk, mv)


def _moba_prompt_kernel(q_ref, k_ref, v_ref, o_ref, kmean_ref, *, nb):
    blk = MOBA_BLOCK
    i = pl.program_id(2)
    scale = A_HEAD_DIM ** -0.5

    @pl.when(i == 0)
    def _():
        kmean_ref[...] = jnp.zeros_like(kmean_ref)
        k = k_ref[0].astype(F32)
        kmean_ref[0:nb, :] = jnp.mean(k.reshape(nb, blk, V7X_LANES), axis=1)

    q = q_ref[0]
    lane = lax.broadcasted_iota(jnp.int32, (1, V7X_LANES), 1)
    kmean = kmean_ref[...].astype(BF16)
    biota = lax.broadcasted_iota(jnp.int32, (blk, V7X_LANES), 1)
    row = lax.broadcasted_iota(jnp.int32, (blk, blk), 0)
    col = lax.broadcasted_iota(jnp.int32, (blk, blk), 1)
    outs = []
    for h in range(2):
        in_head = (lane >= A_HEAD_DIM * h) & (lane < A_HEAD_DIM * (h + 1))
        qh = jnp.where(in_head, q, jnp.zeros_like(q))
        bs = jnp.where(biota < i, _dot_nt(qh, kmean), NEG)
        sel = jnp.zeros(bs.shape, F32)
        for _ in range(MOBA_TOPK):
            mx = jnp.max(bs, axis=-1, keepdims=True)
            idx = jnp.min(jnp.where(bs == mx, biota, 1 << 30), axis=-1, keepdims=True)
            hit = biota == idx
            sel = jnp.where(hit & (mx > 0.5 * NEG), 1.0, sel)
            bs = jnp.where(hit, REMOVED, bs)

        s = _dot_nt(qh, k_ref[0, pl.ds(pl.multiple_of(i * blk, blk), blk), :]) * scale
        s = jnp.where(col <= row, s, NEG)
        m = jnp.max(s, axis=-1, keepdims=True)
        p = jnp.exp(s - m)
        l = jnp.sum(p, axis=-1, keepdims=True)
        acc = _dot(p.astype(BF16), v_ref[0, pl.ds(pl.multiple_of(i * blk, blk), blk), :])

        def body(j, carry, qh=qh, sel=sel):
            m, l, acc = carry
            off = pl.multiple_of(j * blk, blk)
            s = _dot_nt(qh, k_ref[0, pl.ds(off, blk), :]) * scale
            allowed = jnp.sum(jnp.where(biota == j, sel, 0.0), axis=-1, keepdims=True) > 0.5
            s = jnp.where(allowed, s, NEG)
            m_new = jnp.maximum(m, jnp.max(s, axis=-1, keepdims=True))
            alpha = jnp.exp(m - m_new)
            p = jnp.exp(s - m_new)
            l = alpha * l + jnp.sum(p, axis=-1, keepdims=True)
            acc = alpha * acc + _dot(p.astype(BF16), v_ref[0, pl.ds(off, blk), :])
            return m_new, l, acc

        m, l, acc = lax.fori_loop(0, i, body, (m, l, acc))
        outs.append(acc / l)
    o_ref[0] = jnp.where(lane < A_HEAD_DIM, outs[0], outs[1]).astype(o_ref.dtype)


def _moba_prompt(q, k, v):
    n, s, w = q.shape
    blk = MOBA_BLOCK
    nb = s // blk
    assert s % blk == 0 and nb <= V7X_LANES
    hp = w // V7X_LANES
    return pl.pallas_call(
        functools.partial(_moba_prompt_kernel, nb=nb),
        grid=(n, hp, nb),
        in_specs=[pl.BlockSpec((1, blk, V7X_LANES), lambda b, g, i: (b, i, g)),
                  pl.BlockSpec((1, s, V7X_LANES), lambda b, g, i: (b, 0, g)),
                  pl.BlockSpec((1, s, V7X_LANES), lambda b, g, i: (b, 0, g))],
        out_specs=pl.BlockSpec((1, blk, V7X_LANES), lambda b, g, i: (b, i, g)),
        out_shape=jax.ShapeDtypeStruct((n, s, w), BF16),
        scratch_shapes=[pltpu.VMEM((V7X_LANES, V7X_LANES), F32)],
        compiler_params=_params(3),
        name="moba_prompt",
    )(q, k, v)


def _gdn_kernel(xb_ref, c_ref, ct_ref, conv0_ref, m0_ref, cw_ref, avec_ref, dvec_ref, acol_ref, dcol_ref, gout_ref,
                o_ref, convn_ref, mout_ref, xbuf_ref, m_ref, *, chunk, n_chunks, t_valid, prec_solve, prec_mm):
    c = chunk
    ci = pl.program_id(1)
    qkv_w = 3 * B_HEADS * B_HEAD_DIM
    hw = B_HEADS * B_HEAD_DIM
    pad = V7X_SUBLANES

    @pl.when(ci == 0)
    def _():
        xbuf_ref[0:pad, :] = jnp.zeros((pad, qkv_w), F32)
        xbuf_ref[pad - (CONV_W - 1):pad, :] = conv0_ref[0]
        m_ref[...] = m0_ref[0]

    xbuf_ref[pad:pad + c, :] = xb_ref[0, :, 0:qkv_w]
    cw = cw_ref[...]
    y = cw[0:1] * xbuf_ref[pad - 3:pad - 3 + c, :]
    for i in range(1, CONV_W):
        y = y + cw[i:i + 1] * xbuf_ref[pad - 3 + i:pad - 3 + i + c, :]
    y = y * _sigmoid(y)
    last = xbuf_ref[pad - 3 + t_valid:pad + t_valid, :]
    xbuf_ref[pad - 3:pad, :] = last

    @pl.when(ci == n_chunks - 1)
    def _():
        convn_ref[0] = last

    z = xb_ref[0, :, qkv_w:qkv_w + hw]

    cfull = c_ref[0]
    beta_full = _sigmoid(cfull)
    g_full = -jnp.exp(avec_ref[...]) * _softplus(cfull + dvec_ref[...])
    g_rows = -jnp.exp(acol_ref[:, 0:c]) * _softplus(ct_ref[0, 0] + dcol_ref[:, 0:c])
    r_i = lax.broadcasted_iota(jnp.int32, (c, c), 0)
    c_i = lax.broadcasted_iota(jnp.int32, (c, c), 1)
    if t_valid < c:
        live_col = lax.broadcasted_iota(jnp.int32, (c, 1), 0) < t_valid
        live_row = lax.broadcasted_iota(jnp.int32, (1, c), 1) < t_valid
        beta_full = jnp.where(live_col, beta_full, 0.0)
        g_full = jnp.where(live_col, g_full, 0.0)
        g_rows = jnp.where(live_row, g_rows, 0.0)
    incl = c_i <= r_i
    strict = c_i < r_i
    g_cum_cols = _dot(incl.astype(F32), g_full, HI)
    g_cum_rows = _dot(g_rows, (r_i <= c_i).astype(F32), HI)
    eye = (r_i == c_i).astype(F32)
    n_updates = int(round(math.log2(c))) - 1
    gout = gout_ref[...]

    for h in range(B_HEADS):
        sl = slice(h * B_HEAD_DIM, (h + 1) * B_HEAD_DIM)
        qh = y[:, h * B_HEAD_DIM:(h + 1) * B_HEAD_DIM]
        kh = y[:, hw + h * B_HEAD_DIM:hw + (h + 1) * B_HEAD_DIM]
        vh = y[:, 2 * hw + h * B_HEAD_DIM:2 * hw + (h + 1) * B_HEAD_DIM]
        qn = qh * lax.rsqrt(jnp.sum(qh * qh, axis=-1, keepdims=True) + EPS) * (B_HEAD_DIM ** -0.5)
        kn = kh * lax.rsqrt(jnp.sum(kh * kh, axis=-1, keepdims=True) + EPS)
        beta = beta_full[:, h:h + 1]
        gc = g_cum_cols[:, B_HEADS + h:B_HEADS + h + 1]
        gr = g_cum_rows[B_HEADS + h:B_HEADS + h + 1, :]
        decay = jnp.where(incl, jnp.exp(jnp.where(incl, gc - gr, 0.0)), 0.0)
        low = jnp.where(strict, beta * _dot_nt(kn, kn, prec_mm) * decay, 0.0)
        inv = eye - low
        pw = _dot(low, low, prec_solve)
        for u_i in range(n_updates):
            inv = inv + _dot(inv, pw, prec_solve)
            if u_i + 1 < n_updates:
                pw = _dot(pw, pw, prec_solve)
        e_g = jnp.exp(gc)
        w_mat = _dot(inv, (beta * e_g) * kn, prec_solve)
        ub = _dot(inv, beta * vh, prec_solve)
        qk = jnp.where(incl, _dot_nt(qn, kn, prec_mm) * decay, 0.0)
        g_last = gr[:, c - 1:c]
        kend = jnp.exp(g_last - gc) * kn
        m = m_ref[h]
        u = ub - _dot(w_mat, m, prec_mm)
        o = _dot(e_g * qn, m, prec_mm) + _dot(qk, u, prec_mm)
        kend_t = _dot_nt(jnp.eye(B_HEAD_DIM, dtype=F32), kend, HI)
        m_ref[h] = jnp.exp(g_last) * m + _dot(kend_t, u, prec_mm)
        on = o * lax.rsqrt(jnp.mean(o * o, axis=-1, keepdims=True) + EPS) * gout
        zh = z[:, sl]
        o_ref[0, :, sl] = (on * (zh * _sigmoid(zh))).astype(o_ref.dtype)

    @pl.when(ci == n_chunks - 1)
    def _():
        mout_ref[0] = m_ref[...]


def _gdn(xb, cg, conv0, m0, conv_w, a_log, dt_bias, g_out, prec_solve=HI, prec_mm=HI):
    n, t, _ = xb.shape
    c = GDN_CHUNK if t % GDN_CHUNK == 0 else t
    nc = t // c
    qkv_w = 3 * B_HEADS * B_HEAD_DIM
    ct = jnp.swapaxes(cg[:, :, 0:2 * B_HEADS].reshape(n, nc, c, 2 * B_HEADS), 2, 3)
    lane = jnp.arange(V7X_LANES)
    on_g = (lane >= B_HEADS) & (lane < 2 * B_HEADS)
    avec = jnp.where(on_g, jnp.roll(jnp.pad(a_log, (0, V7X_LANES - B_HEADS)), B_HEADS), 0.0).reshape(1, V7X_LANES)
    dvec = jnp.where(on_g, jnp.roll(jnp.pad(dt_bias, (0, V7X_LANES - B_HEADS)), B_HEADS), 0.0).reshape(1, V7X_LANES)
    acol = jnp.broadcast_to(jnp.concatenate([jnp.zeros((B_HEADS,), F32), a_log])[:, None], (2 * B_HEADS, V7X_LANES))
    dcol = jnp.broadcast_to(jnp.concatenate([jnp.zeros((B_HEADS,), F32), dt_bias])[:, None], (2 * B_HEADS, V7X_LANES))
    full = lambda shp: pl.BlockSpec(shp, lambda b, i: (0,) * len(shp))
    kern = functools.partial(_gdn_kernel, chunk=c, n_chunks=nc, t_valid=c, prec_solve=prec_solve, prec_mm=prec_mm)
    return pl.pallas_call(
        kern,
        grid=(n, nc),
        in_specs=[pl.BlockSpec((1, c, xb.shape[2]), lambda b, i: (b, i, 0)),
                  pl.BlockSpec((1, c, V7X_LANES), lambda b, i: (b, i, 0)),
                  pl.BlockSpec((1, 1, 2 * B_HEADS, c), lambda b, i: (b, i, 0, 0)),
                  pl.BlockSpec((1, CONV_W - 1, qkv_w), lambda b, i: (b, 0, 0)),
                  pl.BlockSpec((1, B_HEADS, B_HEAD_DIM, B_HEAD_DIM), lambda b, i: (b, 0, 0, 0)),
                  full((CONV_W, qkv_w)), full((1, V7X_LANES)), full((1, V7X_LANES)),
                  full((2 * B_HEADS, V7X_LANES)), full((2 * B_HEADS, V7X_LANES)), full((1, B_HEAD_DIM))],
        out_specs=[pl.BlockSpec((1, c, B_HEADS * B_HEAD_DIM), lambda b, i: (b, i, 0)),
                   pl.BlockSpec((1, CONV_W - 1, qkv_w), lambda b, i: (b, 0, 0)),
                   pl.BlockSpec((1, B_HEADS, B_HEAD_DIM, B_HEAD_DIM), lambda b, i: (b, 0, 0, 0))],
        out_shape=[jax.ShapeDtypeStruct((n, t, B_HEADS * B_HEAD_DIM), BF16),
                   jax.ShapeDtypeStruct((n, CONV_W - 1, qkv_w), F32),
                   jax.ShapeDtypeStruct((n, B_HEADS, B_HEAD_DIM, B_HEAD_DIM), F32)],
        scratch_shapes=[pltpu.VMEM((V7X_SUBLANES + c, qkv_w), F32),
                        pltpu.VMEM((B_HEADS, B_HEAD_DIM, B_HEAD_DIM), F32)],
        compiler_params=_params(2),
        name="gdn",
    )(xb, cg, ct, conv0, m0, conv_w, avec, dvec, acol, dcol, g_out.reshape(1, B_HEAD_DIM))


def _merge_kernel(oa_ref, ob_ref, om_ref, gl_ref, x_ref, wa_ref, wb_ref, wm_ref, wo_ref, o_ref):
    d = x_ref.shape[-1]
    merged = (_sigmoid(gl_ref[:, 0:d]) * _dot(oa_ref[...], wa_ref[...])
              + _sigmoid(gl_ref[:, d:2 * d]) * _dot(ob_ref[...], wb_ref[...])
              + _sigmoid(gl_ref[:, 2 * d:3 * d]) * _dot(om_ref[...], wm_ref[...]))
    o_ref[...] = x_ref[...] + _dot(merged.astype(BF16), wo_ref[...])


def _merge(oa, ob, om, gl, x, w_oa, w_ob, w_om, w_out, tm=256):
    t, d = x.shape
    tm = _tile(t, tm)
    bw = oa.shape[1]
    row = lambda w: pl.BlockSpec((tm, w), lambda i: (i, 0))
    full = lambda a: pl.BlockSpec(a.shape, lambda i: (0, 0))
    return pl.pallas_call(
        _merge_kernel,
        grid=(t // tm,),
        in_specs=[row(bw), row(bw), row(bw), row(N_BRANCH * d), row(d), full(w_oa), full(w_ob), full(w_om), full(w_out)],
        out_specs=row(d),
        out_shape=jax.ShapeDtypeStruct((t, d), F32),
        compiler_params=_params(1),
        name="merge",
    )(oa, ob, om, gl, x, w_oa, w_ob, w_om, w_out)


def _topk_axis0(s, k, iota):
    vals, idxs = [], []
    for _ in range(k):
        mx = jnp.max(s, axis=0, keepdims=True)
        idx = jnp.min(jnp.where(s == mx, iota, 1 << 30), axis=0, keepdims=True)
        vals.append(mx)
        idxs.append(idx)
        s = jnp.where(iota == idx, REMOVED, s)
    return vals, idxs


def _peer_route_kernel(x_ref, g_ref, wq_ref, k1_ref, k2_ref, hn_ref, eidx_ref, gate_ref):
    x = x_ref[...]
    tb = x.shape[0]
    hn = x * lax.rsqrt(jnp.mean(x * x, axis=-1, keepdims=True) + EPS) * g_ref[...]
    hn_ref[...] = hn
    q = _dot(hn.astype(BF16), wq_ref[...])
    kk = PEER_TOPK
    key_iota = lax.broadcasted_iota(jnp.int32, (PEER_NKEYS, tb), 0)
    cand_iota = lax.broadcasted_iota(jnp.int32, (kk * kk, tb), 0)
    for h in range(PEER_HEADS):
        q1 = q[:, 2 * h * PEER_HALF:(2 * h + 1) * PEER_HALF].astype(BF16)
        q2 = q[:, (2 * h + 1) * PEER_HALF:(2 * h + 2) * PEER_HALF].astype(BF16)
        v1, i1 = _topk_axis0(_dot_nt(k1_ref[h], q1), kk, key_iota)
        v2, i2 = _topk_axis0(_dot_nt(k2_ref[h], q2), kk, key_iota)
        v2a = jnp.concatenate(v2, axis=0)
        i2a = jnp.concatenate(i2, axis=0)
        cand = jnp.concatenate([v1[a] + v2a for a in range(kk)], axis=0)
        cidx = jnp.concatenate([i1[a] * PEER_NKEYS + i2a for a in range(kk)], axis=0)
        sc, es = [], []
        for _ in range(kk):
            mx = jnp.max(cand, axis=0, keepdims=True)
            pos = jnp.min(jnp.where(cand == mx, cand_iota, 1 << 30), axis=0, keepdims=True)
            hit = cand_iota == pos
            es.append(jnp.sum(jnp.where(hit, cidx, 0), axis=0, keepdims=True))
            sc.append(mx)
            cand = jnp.where(hit, REMOVED, cand)
        ex = [jnp.exp(s - sc[0]) for s in sc]
        den = ex[0]
        for e in ex[1:]:
            den = den + e
        eidx_ref[h * kk:(h + 1) * kk, :] = jnp.concatenate(es, axis=0)
        gate_ref[h * kk:(h + 1) * kk, :] = jnp.concatenate(ex, axis=0) / den


def _peer_route(x, g, w_pq, k1, k2, tb=256):
    t, d = x.shape
    tb = _tile(t, tb)
    npair = PEER_HEADS * PEER_TOPK
    return pl.pallas_call(
        _peer_route_kernel,
        grid=(t // tb,),
        in_specs=[pl.BlockSpec((tb, d), lambda i: (i, 0)), pl.BlockSpec((1, d), lambda i: (0, 0)),
                  pl.BlockSpec(w_pq.shape, lambda i: (0, 0)),
                  pl.BlockSpec(k1.shape, lambda i: (0, 0, 0)), pl.BlockSpec(k2.shape, lambda i: (0, 0, 0))],
        out_specs=[pl.BlockSpec((tb, d), lambda i: (i, 0)), pl.BlockSpec((npair, tb), lambda i: (0, i)),
                   pl.BlockSpec((npair, tb), lambda i: (0, i))],
        out_shape=[jax.ShapeDtypeStruct((t, d), F32), jax.ShapeDtypeStruct((npair, t), jnp.int32),
                   jax.ShapeDtypeStruct((npair, t), F32)],
        compiler_params=_params(1),
        name="peer_route",
    )(x, g.reshape(1, d), w_pq, k1, k2)


def _expert_row(tab_ref, e):
    tile = tab_ref[e >> 1].astype(F32)
    return jnp.where((e & 1) == 1, tile[V7X_SUBLANES:2 * V7X_SUBLANES], tile[0:V7X_SUBLANES])


def _gelu(x):
    return 0.5 * x * (1.0 + lax.erf(x * (2.0 ** -0.5)))


def _peer_u_kernel(idx_ref, hn_ref, gate_ref, tab_ref, w_ref, d_ref):
    tb, npair = gate_ref.shape
    ones = jnp.ones((V7X_SUBLANES, V7X_LANES), F32)

    def token(t, carry):
        x = hn_ref[t]
        base = t * npair
        for p in range(npair):
            u = _expert_row(tab_ref, idx_ref[base + p])
            d_ref[p:p + 1, :] = jnp.sum(u * x, axis=0, keepdims=True)
        dots = _dot_nt(ones, d_ref[...], HI)[0:1]
        w_ref[pl.ds(t, 1), :] = gate_ref[pl.ds(t, 1), :] * _gelu(dots)
        return carry

    lax.fori_loop(0, tb, token, 0)


PEER_TB = 64


def _peer_u(hn3, eidx, gate, utab, tb=PEER_TB):
    t = hn3.shape[0]
    npair = gate.shape[1]
    return pl.pallas_call(
        _peer_u_kernel,
        grid=(t // tb,),
        in_specs=[pl.BlockSpec((tb * npair,), lambda i: (i,), memory_space=pltpu.SMEM),
                  pl.BlockSpec((tb,) + hn3.shape[1:], lambda i: (i, 0, 0)),
                  pl.BlockSpec((tb, npair), lambda i: (i, 0)),
                  pl.BlockSpec(utab.shape, lambda i: (0, 0, 0), pipeline_mode=pl.Buffered(1))],
        out_specs=pl.BlockSpec((tb, npair), lambda i: (i, 0)),
        out_shape=jax.ShapeDtypeStruct((t, npair), F32),
        scratch_shapes=[pltpu.VMEM((npair, V7X_LANES), F32)],
        compiler_params=_params(1),
        name="peer_u",
    )(eidx, hn3, gate, utab)


def _peer_v_kernel(idx_ref, w_ref, tab_ref, o_ref):
    tb = o_ref.shape[0]
    npair = idx_ref.shape[0] // tb
    n_acc = 4

    def token(t, carry):
        base = t * npair
        accs = [jnp.zeros((V7X_SUBLANES, V7X_LANES), F32) for _ in range(n_acc)]
        for p in range(npair):
            accs[p % n_acc] = accs[p % n_acc] + w_ref[base + p] * _expert_row(tab_ref, idx_ref[base + p])
        o_ref[t] = (accs[0] + accs[1]) + (accs[2] + accs[3])
        return carry

    lax.fori_loop(0, tb, token, 0)


def _peer_v(eidx, w, vtab, tb=PEER_TB):
    t = w.shape[0] // (PEER_HEADS * PEER_TOPK)
    npair = PEER_HEADS * PEER_TOPK
    return pl.pallas_call(
        _peer_v_kernel,
        grid=(t // tb,),
        in_specs=[pl.BlockSpec((tb * npair,), lambda i: (i,), memory_space=pltpu.SMEM),
                  pl.BlockSpec((tb * npair,), lambda i: (i,), memory_space=pltpu.SMEM),
                  pl.BlockSpec(vtab.shape, lambda i: (0, 0, 0), pipeline_mode=pl.Buffered(1))],
        out_specs=pl.BlockSpec((tb, V7X_SUBLANES, V7X_LANES), lambda i: (i, 0, 0)),
        out_shape=jax.ShapeDtypeStruct((t, V7X_SUBLANES, V7X_LANES), F32),
        compiler_params=_params(1),
        name="peer_v",
    )(eidx, w, vtab)


def _expert_table(tab):
    e, d = tab.shape
    return tab.astype(BF16).reshape(e // 2, 2 * d // V7X_LANES, V7X_LANES)


def _peer(x, g, w_pq, k1, k2, u_emb, v_emb):
    t, d = x.shape
    hn, eidx_t, gate_t = _peer_route(x, g, w_pq.astype(BF16), k1.astype(BF16), k2.astype(BF16))
    eidx = eidx_t.T.reshape(-1)
    w = _peer_u(hn.reshape(t, d // V7X_LANES, V7X_LANES), eidx, gate_t.T, _expert_table(u_emb))
    out = _peer_v(eidx, w.reshape(-1), _expert_table(v_emb))
    return out.reshape(t, d)


PAGE = 128
PAGES_PER_BLOCK = MOBA_BLOCK // PAGE
MEAN_GROUP = 8


def _page_mean_kernel(pt_ref, *refs):
    del pt_ref
    pages, o_ref = refs[:-1], refs[-1]
    for r, page in enumerate(pages):
        o_ref[0, r] = jnp.mean(page[0], axis=1)


def _page_means(pool_k, page_table):
    n, n_pages = page_table.shape
    _, h, ps, dh = pool_k.shape
    g = MEAN_GROUP

    def page_spec(r):
        return pl.BlockSpec((1, h, ps, dh), lambda b, i, pt: (pt[b * n_pages + i * g + r], 0, 0, 0))

    return pl.pallas_call(
        _page_mean_kernel,
        grid_spec=pltpu.PrefetchScalarGridSpec(
            num_scalar_prefetch=1,
            grid=(n, n_pages // g),
            in_specs=[page_spec(r) for r in range(g)],
            out_specs=pl.BlockSpec((1, g, h, dh), lambda b, i, pt: (b, i, 0, 0)),
        ),
        out_shape=jax.ShapeDtypeStruct((n, n_pages, h, dh), F32),
        compiler_params=_params(2),
        name="moba_page_means",
    )(page_table.reshape(-1), *([pool_k] * g))


def _moba_select_kernel(q_ref, pm_ref, sel_ref):
    n_pages = pm_ref.shape[2]
    n_blocks = n_pages // PAGES_PER_BLOCK
    r_i = lax.broadcasted_iota(jnp.int32, (V7X_LANES, n_pages), 0)
    c_i = lax.broadcasted_iota(jnp.int32, (V7X_LANES, n_pages), 1)
    avg = jnp.where(c_i // PAGES_PER_BLOCK == r_i, 1.0 / PAGES_PER_BLOCK, 0.0)
    t = q_ref.shape[1]
    biota = lax.broadcasted_iota(jnp.int32, (t, V7X_LANES), 1)
    for h in range(A_HEADS):
        qh = q_ref[0, :, h * A_HEAD_DIM:(h + 1) * A_HEAD_DIM].astype(BF16)
        pmb = _dot(avg, pm_ref[0, h], HI)
        bs = jnp.where(biota < n_blocks, _dot_nt(qh, pmb.astype(BF16)), NEG)
        out = jnp.zeros((t, V7X_LANES), jnp.int32)
        for r in range(MOBA_TOPK):
            mx = jnp.max(bs, axis=-1, keepdims=True)
            idx = jnp.min(jnp.where(bs == mx, biota, 1 << 30), axis=-1, keepdims=True)
            out = jnp.where(biota == r, idx, out)
            bs = jnp.where(biota == idx, REMOVED, bs)
        sel_ref[0, h] = out


def _moba_select(q, pm):
    n, t, w = q.shape
    return pl.pallas_call(
        _moba_select_kernel,
        grid=(n,),
        in_specs=[pl.BlockSpec((1, t, w), lambda b: (b, 0, 0)),
                  pl.BlockSpec((1,) + pm.shape[1:], lambda b: (b, 0, 0, 0))],
        out_specs=pl.BlockSpec((1, A_HEADS, t, V7X_LANES), lambda b: (b, 0, 0, 0)),
        out_shape=jax.ShapeDtypeStruct((n, A_HEADS, t, V7X_LANES), jnp.int32),
        compiler_params=_params(1),
        name="moba_select",
    )(q, pm)


def _moba_sample_kernel(pt_ref, sel_ref, q_ref, kn_ref, vn_ref, *refs):
    del pt_ref, sel_ref
    t = q_ref.shape[2]
    n_sel_pages = MOBA_TOPK * PAGES_PER_BLOCK
    k_pages = refs[:t * n_sel_pages]
    v_pages = refs[t * n_sel_pages:2 * t * n_sel_pages]
    o_ref = refs[-1]
    scale = A_HEAD_DIM ** -0.5
    kn = kn_ref[0, 0]
    vn = vn_ref[0, 0]
    tok = lax.broadcasted_iota(jnp.int32, (t, 1), 0)
    for ti in range(t):
        qt = q_ref[0, 0, ti:ti + 1, :] * scale
        s_new = jnp.where(tok <= ti, jnp.sum(kn * qt, axis=-1, keepdims=True), NEG)
        s_pg = [jnp.sum(k_pages[ti * n_sel_pages + j][0, 0] * qt, axis=-1, keepdims=True) for j in range(n_sel_pages)]
        m = jnp.max(s_new, axis=0, keepdims=True)
        for s in s_pg:
            m = jnp.maximum(m, jnp.max(s, axis=0, keepdims=True))
        p_new = jnp.exp(s_new - m)
        l = jnp.sum(p_new, axis=0, keepdims=True)
        o = jnp.sum(p_new * vn, axis=0, keepdims=True)
        for j, s in enumerate(s_pg):
            p = jnp.exp(s - m)
            l = l + jnp.sum(p, axis=0, keepdims=True)
            o = o + jnp.sum(p * v_pages[ti * n_sel_pages + j][0, 0], axis=0, keepdims=True)
        o_ref[0, 0, ti:ti + 1, :] = (o / l).astype(o_ref.dtype)


def _moba_sample(q, kn, vn, pool_k, pool_v, page_table, sel):
    n, h, t, dh = q.shape
    n_pages = page_table.shape[1]

    def page_spec(ti, s, pp):
        def index(b, hh, pt, sl):
            blk = sl[((b * h + hh) * t + ti) * MOBA_TOPK + s]
            return (pt[b * n_pages + blk * PAGES_PER_BLOCK + pp], hh, 0, 0)
        return pl.BlockSpec((1, 1, PAGE, dh), index)

    page_specs = [page_spec(ti, s, pp) for ti in range(t) for s in range(MOBA_TOPK) for pp in range(PAGES_PER_BLOCK)]
    new_spec = pl.BlockSpec((1, 1, t, dh), lambda b, hh, pt, sl: (b, hh, 0, 0))
    return pl.pallas_call(
        _moba_sample_kernel,
        grid_spec=pltpu.PrefetchScalarGridSpec(
            num_scalar_prefetch=2,
            grid=(n, h),
            in_specs=[new_spec, new_spec, new_spec] + page_specs + page_specs,
            out_specs=new_spec,
        ),
        out_shape=jax.ShapeDtypeStruct((n, h, t, dh), F32),
        compiler_params=_params(2),
        name="moba_sample",
    )(page_table.reshape(-1), sel.reshape(-1), q, kn, vn,
      *([pool_k] * len(page_specs)), *([pool_v] * len(page_specs)))


def kernel(x_prompt, x_sample, cache_moba_k, cache_moba_v, state_gdn, state_conv, cache_mem_k, cache_mem_v, page_table, mem_prompt, w_in, g_attn, conv_w, a_log, dt_bias, g_gdn_out, w_oa, w_ob, w_om, w_out, g_mem, w_mem_kv, g_ffn, w_peer_q, peer_k1, peer_k2, peer_u, peer_v, g_final):
    depth = w_in.shape[0]
    n_p, t_p, d = x_prompt.shape
    n_s, t_s, _ = x_sample.shape
    tp, ts = n_p * t_p, n_s * t_s
    n_mem = mem_prompt.shape[1]
    a_w, b_w, m_w = A_HEADS * A_HEAD_DIM, B_HEADS * B_HEAD_DIM, M_HEADS * M_HEAD_DIM
    c0 = 3 * a_w
    c1 = c0 + 4 * b_w
    c2 = c1 + 2 * B_HEADS
    c3 = c2 + m_w
    assert page_table.shape[1] % PAGES_PER_BLOCK == 0 and t_p % MOBA_BLOCK == 0
    x = jnp.concatenate([x_prompt.reshape(tp, d), x_sample.reshape(ts, d)], axis=0)
    outs = [[] for _ in range(10)]
    for l in range(depth):
        wl = w_in[l].astype(BF16)
        h = _rmsnorm(x, g_attn[l], BF16)
        pa = _matmul(h, wl[:, :c0], F32)
        pb = _matmul(h, wl[:, c0:c1], F32)
        pc = _matmul(h, jnp.pad(wl[:, c1:c2], ((0, 0), (0, V7X_LANES - 2 * B_HEADS))), F32)
        pm = _matmul(h, wl[:, c2:c3], BF16)
        pg = _matmul(h, wl[:, c3:], F32)

        ap = pa[:tp].reshape(n_p, t_p, c0)
        ak_p, av_p = ap[..., a_w:2 * a_w], ap[..., 2 * a_w:]
        oa_p = _moba_prompt(ap[..., :a_w].astype(BF16), ak_p.astype(BF16), av_p.astype(BF16))
        hm = _rmsnorm(mem_prompt.reshape(n_p * n_mem, d), g_mem[l], BF16)
        mkv = _matmul(hm, w_mem_kv[l].astype(BF16), F32)
        mk_p, mv_p = mkv[:, :m_w].reshape(n_p, n_mem, m_w), mkv[:, m_w:].reshape(n_p, n_mem, m_w)
        om_p = _mem_attn(pm[:tp].reshape(n_p, t_p, m_w), mk_p.astype(BF16), mv_p.astype(BF16), 512)
        ob_p, conv_p, st_p = _gdn(pb[:tp].reshape(n_p, t_p, 4 * b_w), pc[:tp].reshape(n_p, t_p, V7X_LANES),
                                  jnp.zeros((n_p, CONV_W - 1, 3 * b_w), F32),
                                  jnp.zeros((n_p, B_HEADS, B_HEAD_DIM, B_HEAD_DIM), F32),
                                  conv_w[l], a_log[l], dt_bias[l], g_gdn_out[l])
        to_pages = lambda a: a.reshape(n_p, t_p // PAGE, PAGE, A_HEADS, A_HEAD_DIM).transpose(0, 1, 3, 2, 4)
        outs[0].append(to_pages(ak_p))
        outs[1].append(to_pages(av_p))
        outs[2].append(st_p)
        outs[3].append(conv_p)
        outs[4].append(mk_p.reshape(n_p, n_mem, M_HEADS, M_HEAD_DIM))
        outs[5].append(mv_p.reshape(n_p, n_mem, M_HEADS, M_HEAD_DIM))

        as_ = pa[tp:].reshape(n_s, t_s, c0)
        heads = lambda a: a.reshape(n_s, t_s, A_HEADS, A_HEAD_DIM).transpose(0, 2, 1, 3)
        aq_s, ak_s, av_s = heads(as_[..., :a_w]), heads(as_[..., a_w:2 * a_w]), heads(as_[..., 2 * a_w:])
        means = _page_means(cache_moba_k[l], page_table)
        sel = _moba_select(as_[..., :a_w], jnp.swapaxes(means, 1, 2))[..., :MOBA_TOPK]
        oa_s = _moba_sample(aq_s, ak_s, av_s, cache_moba_k[l], cache_moba_v[l], page_table, sel)
        oa_s = oa_s.transpose(0, 2, 1, 3).reshape(ts, a_w).astype(BF16)
        om_s = _mem_attn(pm[tp:].reshape(n_s, t_s, m_w), cache_mem_k[l].reshape(n_s, n_mem, m_w).astype(BF16),
                         cache_mem_v[l].reshape(n_s, n_mem, m_w).astype(BF16), t_s)
        ob_s, conv_s, st_s = _gdn(pb[tp:].reshape(n_s, t_s, 4 * b_w), pc[tp:].reshape(n_s, t_s, V7X_LANES),
                                  state_conv[l], state_gdn[l], conv_w[l], a_log[l], dt_bias[l], g_gdn_out[l])
        outs[6].append(ak_s)
        outs[7].append(av_s)
        outs[8].append(st_s)
        outs[9].append(conv_s)

        oa = jnp.concatenate([oa_p.reshape(tp, a_w), oa_s], axis=0)
        ob = jnp.concatenate([ob_p.reshape(tp, b_w), ob_s.reshape(ts, b_w)], axis=0)
        om = jnp.concatenate([om_p.reshape(tp, m_w), om_s.reshape(ts, m_w)], axis=0)
        x = _merge(oa, ob, om, pg, x, w_oa[l].astype(BF16), w_ob[l].astype(BF16), w_om[l].astype(BF16),
                   w_out[l].astype(BF16))
        ffn = _peer(x, g_ffn[l], w_peer_q[l], peer_k1[l], peer_k2[l], peer_u[l], peer_v[l])
        if l + 1 < depth:
            x = x + ffn
    y = _add_rmsnorm(x, ffn, g_final)
    stacked = [jnp.stack(o, axis=0) for o in outs]
    return (y[:tp].reshape(n_p, t_p, d), y[tp:].reshape(n_s, t_s, d), *stacked)
```
